```python
import math
import jax, jax.numpy as jnp
from jax import lax
import numpy as np

D_MODEL = 1024
BATCH = 4
SEQ = 8192
DEPTH = 1

D_MIX = D_MODEL
D_POOL = D_MIX // 2
POOL_WINDOWS = (2, 4, 8, 16)
N_POOL_GROUPS = len(POOL_WINDOWS)
POOL_GROUP = D_POOL // N_POOL_GROUPS
D_GMLP = D_MIX - D_POOL
N_GMLP_HEADS = 8
GMLP_HEAD = D_GMLP // N_GMLP_HEADS
CHUNK = 128
D_FF = 2816
N_SUB = 3
N_MOD = 3
EPS = 1e-6
HALF_STEP = 0.5

kernel_name = "hybrid_pool_gmlp_macaron_adaln"


def rms_norm(x, g):
    xf = x.astype(jnp.float32)
    r = lax.rsqrt(jnp.mean(xf * xf, axis=-1, keepdims=True) + EPS)
    return (xf * r).astype(x.dtype) * g


def layer_norm(x, g, b):
    xf = x.astype(jnp.float32)
    mu = jnp.mean(xf, axis=-1, keepdims=True)
    var = jnp.mean(jnp.square(xf - mu), axis=-1, keepdims=True)
    return ((xf - mu) * lax.rsqrt(var + EPS)).astype(x.dtype) * g + b


def modulate(h, shift, scale):
    return h * (1.0 + scale[:, None, :]) + shift[:, None, :]


def swiglu(h, w_in, w_out):
    gu = h @ w_in
    g, u = jnp.split(gu, 2, axis=-1)
    return (jax.nn.silu(g) * u) @ w_out


def causal_multiscale_pool(xp, w_pool, pool_scale):
    B, S, _ = xp.shape
    xg = xp.reshape(B, S, N_POOL_GROUPS, POOL_GROUP)
    xf = xg.astype(jnp.float32)
    cs = jnp.cumsum(xf, axis=1)
    pos = jnp.arange(S, dtype=jnp.int32)
    outs = []
    for i, w in enumerate(POOL_WINDOWS):
        c_i = cs[:, :, i]
        c_prev = jnp.pad(c_i[:, :-w], ((0, 0), (w, 0), (0, 0)))
        cnt = jnp.minimum(pos + 1, w).astype(jnp.float32)[None, :, None]
        outs.append((c_i - c_prev) / cnt - xf[:, :, i])
    pooled = jnp.stack(outs, axis=2).astype(xp.dtype)
    mixed = jnp.einsum('bsgc,gcd->bsgd', pooled, w_pool)
    return mixed.reshape(B, S, D_POOL) * pool_scale


def chunked_spatial_gating(zb, ln_g, ln_b, w_spatial, b_spatial):
    B, S, _ = zb.shape
    z = jax.nn.gelu(zb)
    u, v = jnp.split(z, 2, axis=-1)
    v = layer_norm(v, ln_g, ln_b)
    vc = v.reshape(B, S // CHUNK, CHUNK, N_GMLP_HEADS, GMLP_HEAD)
    mask = jnp.tril(jnp.ones((CHUNK, CHUNK), dtype=bool))
    ws = jnp.where(mask[None], w_spatial, jnp.zeros((), w_spatial.dtype))
    sv = jnp.einsum('hts,bnshc->bnthc', ws, vc)
    sv = sv + jnp.transpose(b_spatial)[None, None, :, :, None]
    return u * sv.reshape(B, S, D_GMLP)


def setup_inputs(seed: int = 0) -> dict:
    key = jax.random.key(seed)
    ks = jax.random.split(key, 24)
    f32 = jnp.float32
    L, D = DEPTH, D_MODEL

    def nrm(k, shape, fan_in):
        return jax.random.normal(k, shape, f32) * (fan_in ** -0.5)

    def gain(k, shape):
        return 1.0 + 0.05 * jax.random.normal(k, shape, f32)

    return {
        "x": jax.random.normal(ks[0], (BATCH, SEQ, D), f32),
        "c": jax.random.normal(ks[1], (BATCH, D), f32),
        "w_ada": nrm(ks[2], (L, D, N_SUB * N_MOD * D), D),
        "b_ada": 0.02 * jax.random.normal(ks[3], (L, N_SUB * N_MOD * D), f32),
        "norm_ffn1_g": gain(ks[4], (L, D)),
        "ffn1_w_in": nrm(ks[5], (L, D, 2 * D_FF), D),
        "ffn1_w_out": nrm(ks[6], (L, D_FF, D), D_FF),
        "norm_mix_g": gain(ks[7], (L, D)),
        "w_mix_in": nrm(ks[8], (L, D, D_POOL + 2 * D_GMLP), D),
        "w_pool": nrm(ks[9], (L, N_POOL_GROUPS, POOL_GROUP, POOL_GROUP), POOL_GROUP),
        "pool_scale": gain(ks[10], (L, D_POOL)),
        "gmlp_ln_g": gain(ks[11], (L, D_GMLP)),
        "gmlp_ln_b": 0.02 * jax.random.normal(ks[12], (L, D_GMLP), f32),
        "w_spatial": nrm(ks[13], (L, N_GMLP_HEADS, CHUNK, CHUNK), CHUNK),
        "b_spatial": gain(ks[14], (L, N_GMLP_HEADS, CHUNK)),
        "w_mix_out": nrm(ks[15], (L, D_MIX, D), D_MIX),
        "norm_ffn2_g": gain(ks[16], (L, D)),
        "ffn2_w_in": nrm(ks[17], (L, D, 2 * D_FF), D),
        "ffn2_w_out": nrm(ks[18], (L, D_FF, D), D_FF),
        "norm_final_g": gain(ks[19], (D,)),
    }


def reference(x, c, w_ada, b_ada, norm_ffn1_g, ffn1_w_in, ffn1_w_out,
              norm_mix_g, w_mix_in, w_pool, pool_scale, gmlp_ln_g, gmlp_ln_b,
              w_spatial, b_spatial, w_mix_out, norm_ffn2_g, ffn2_w_in, ffn2_w_out,
              norm_final_g):
    B = x.shape[0]
    c_act = jax.nn.silu(c)
    for l in range(DEPTH):
        mod = (c_act @ w_ada[l] + b_ada[l]).reshape(B, N_SUB, N_MOD, D_MODEL)

        h = modulate(rms_norm(x, norm_ffn1_g[l]), mod[:, 0, 0], mod[:, 0, 1])
        x = x + HALF_STEP * mod[:, 0, 2][:, None, :] * swiglu(h, ffn1_w_in[l], ffn1_w_out[l])

        h = modulate(rms_norm(x, norm_mix_g[l]), mod[:, 1, 0], mod[:, 1, 1])
        proj = h @ w_mix_in[l]
        xa = proj[..., :D_POOL]
        zb = proj[..., D_POOL:]
        ya = causal_multiscale_pool(xa, w_pool[l], pool_scale[l])
        yb = chunked_spatial_gating(zb, gmlp_ln_g[l], gmlp_ln_b[l], w_spatial[l], b_spatial[l])
        mix = jnp.concatenate([ya, yb], axis=-1) @ w_mix_out[l]
        x = x + mod[:, 1, 2][:, None, :] * mix

        h = modulate(rms_norm(x, norm_ffn2_g[l]), mod[:, 2, 0], mod[:, 2, 1])
        x = x + HALF_STEP * mod[:, 2, 2][:, None, :] * swiglu(h, ffn2_w_in[l], ffn2_w_out[l])
    return rms_norm(x, norm_final_g)
```

```python
import functools

import jax
import jax.numpy as jnp
from jax import lax
from jax.experimental import pallas as pl
from jax.experimental.pallas import tpu as pltpu

D_MODEL = 1024
D_FF = 2816
D_POOL = 512
D_GMLP = 512
POOL_WINDOWS = (2, 4, 8, 16)
POOL_GROUP = 128
N_HEADS = 8
HEAD = 64
CHUNK = 128
N_SUB = 3
N_MOD = 3
EPS = 1e-6
HALF_STEP = 0.5

HALO = 16
TM_FFN = 512
TM_MIX = 512
FF_CHUNK = 256
ADA_TN = 1024
VMEM_LIMIT_BYTES = 56 * 1024 * 1024

_F32 = jnp.float32
_BF16 = jnp.bfloat16


def _dot(a, b):
    return jnp.dot(a, b, preferred_element_type=_F32)


def _rms_mod(x, g, shift, scale):
    r = lax.rsqrt(jnp.mean(x * x, axis=-1, keepdims=True) + EPS)
    return ((x * r) * g) * (1.0 + scale) + shift


def _ada_kernel(c_ref, w_ref, b_ref, o_ref):
    c = c_ref[...]
    c_act = (c * jax.nn.sigmoid(c)).astype(_BF16)
    o_ref[...] = _dot(c_act, w_ref[...].astype(_BF16)) + b_ref[...]


def _ada_call(c, w, b):
    bsz, d = c.shape
    n = w.shape[1]
    return pl.pallas_call(
        _ada_kernel,
        grid=(n // ADA_TN,),
        in_specs=[
            pl.BlockSpec((bsz, d), lambda j: (0, 0)),
            pl.BlockSpec((d, ADA_TN), lambda j: (0, j)),
            pl.BlockSpec((1, ADA_TN), lambda j: (0, j)),
        ],
        out_specs=pl.BlockSpec((bsz, ADA_TN), lambda j: (0, j)),
        out_shape=jax.ShapeDtypeStruct((bsz, n), _F32),
        compiler_params=pltpu.CompilerParams(dimension_semantics=("arbitrary",)),
        name="adaln",
    )(c, w, b.reshape(1, n))


def _ffn_kernel(x_ref, mod_ref, g_ref, win_ref, wout_ref, gf_ref, o_ref, a_ref, *, sub, final):
    x = x_ref[...]
    shift = mod_ref[sub * N_MOD + 0:sub * N_MOD + 1, :]
    scale = mod_ref[sub * N_MOD + 1:sub * N_MOD + 2, :]
    gate = mod_ref[sub * N_MOD + 2:sub * N_MOD + 3, :]
    h = _rms_mod(x, g_ref[...], shift, scale).astype(_BF16)
    for k in range(D_FF // FF_CHUNK):
        lo = k * FF_CHUNK
        g = _dot(h, win_ref[:, lo:lo + FF_CHUNK])
        u = _dot(h, win_ref[:, D_FF + lo:D_FF + lo + FF_CHUNK])
        a_ref[:, lo:lo + FF_CHUNK] = ((g * jax.nn.sigmoid(g)) * u).astype(_BF16)
    y = x + (HALF_STEP * gate) * _dot(a_ref[...], wout_ref[...])
    if final:
        r = lax.rsqrt(jnp.mean(y * y, axis=-1, keepdims=True) + EPS)
        y = (y * r) * gf_ref[...]
    o_ref[...] = y


def _ffn_call(x, mod, g, w_in, w_out, g_final, *, sub, final):
    bsz, s, d = x.shape
    const = lambda b, j: (0, 0)
    return pl.pallas_call(
        functools.partial(_ffn_kernel, sub=sub, final=final),
        grid=(bsz, s // TM_FFN),
        in_specs=[
            pl.BlockSpec((None, TM_FFN, d), lambda b, j: (b, j, 0)),
            pl.BlockSpec((None, N_SUB * N_MOD, d), lambda b, j: (b, 0, 0)),
            pl.BlockSpec((1, d), const),
            pl.BlockSpec((d, 2 * D_FF), const, pipeline_mode=pl.Buffered(1)),
            pl.BlockSpec((D_FF, d), const, pipeline_mode=pl.Buffered(1)),
            pl.BlockSpec((1, d), const),
        ],
        out_specs=pl.BlockSpec((None, TM_FFN, d), lambda b, j: (b, j, 0)),
        out_shape=jax.ShapeDtypeStruct(x.shape, _F32),
        scratch_shapes=[pltpu.VMEM((TM_FFN, D_FF), _BF16)],
        compiler_params=pltpu.CompilerParams(
            dimension_semantics=("arbitrary", "arbitrary"),
            vmem_limit_bytes=VMEM_LIMIT_BYTES),
        name="ffn_final" if final else "ffn",
    )(x, mod, g.reshape(1, d), w_in, w_out, g_final.reshape(1, d))


def _mix_kernel(x_ref, mod_ref, g_ref, win_ref, wpool_ref, pscale_ref, lng_ref, lnb_ref,
                ws_ref, bsp_ref, wout_ref, o_ref, e_ref, y_ref):
    j = pl.program_id(1)
    tm = x_ref.shape[0]
    x = x_ref[...]
    shift = mod_ref[N_MOD + 0:N_MOD + 1, :]
    scale = mod_ref[N_MOD + 1:N_MOD + 2, :]
    gate = mod_ref[N_MOD + 2:N_MOD + 3, :]
    h = _rms_mod(x, g_ref[...], shift, scale).astype(_BF16)

    @pl.when(j == 0)
    def _():
        e_ref[0:HALO, :] = jnp.zeros((HALO, D_POOL), _F32)

    e_ref[HALO:, :] = _dot(h, win_ref[:, 0:D_POOL])
    pos = j * tm + lax.broadcasted_iota(jnp.int32, (tm, 1), 0)
    for gi, w in enumerate(POOL_WINDOWS):
        lanes = slice(gi * POOL_GROUP, (gi + 1) * POOL_GROUP)
        e = e_ref[:, lanes]
        s = e
        step = 1
        while step < w:
            s = s + pltpu.roll(s, step, 0)
            step *= 2
        cnt = jnp.minimum(pos + 1, w).astype(_F32)
        pooled = s[HALO:, :] / cnt - e[HALO:, :]
        mixed = _dot(pooled.astype(_BF16), wpool_ref[gi])
        y_ref[:, lanes] = (mixed * pscale_ref[:, lanes]).astype(_BF16)
    e_ref[0:HALO, :] = e_ref[tm:tm + HALO, :]

    z = jax.nn.gelu(_dot(h, win_ref[:, D_POOL:]), approximate=True)
    u = z[:, :D_GMLP]
    v = z[:, D_GMLP:]
    mu = jnp.mean(v, axis=-1, keepdims=True)
    vc = v - mu
    var = jnp.mean(vc * vc, axis=-1, keepdims=True)
    vn = ((vc * lax.rsqrt(var + EPS)) * lng_ref[...] + lnb_ref[...]).astype(_BF16)

    row = lax.broadcasted_iota(jnp.int32, (CHUNK, N_HEADS * CHUNK), 0)
    col = lax.broadcasted_iota(jnp.int32, (CHUNK, N_HEADS * CHUNK), 1)
    ws = jnp.where((col % CHUNK) <= row, ws_ref[...], 0.0).astype(_BF16)
    lane = lax.broadcasted_iota(jnp.int32, (CHUNK, 2 * HEAD), 1)
    first_head = lane < HEAD
    zero = jnp.zeros((CHUNK, 2 * HEAD), _BF16)
    for n in range(tm // CHUNK):
        rows = slice(n * CHUNK, (n + 1) * CHUNK)
        for p in range(N_HEADS // 2):
            lanes = slice(p * 2 * HEAD, (p + 1) * 2 * HEAD)
            vp = vn[rows, lanes]
            rhs = jnp.concatenate([jnp.where(first_head, vp, zero),
                                   jnp.where(first_head, zero, vp)], axis=0)
            sv = _dot(ws[:, p * 2 * CHUNK:(p + 1) * 2 * CHUNK], rhs)
            yb = u[rows, lanes] * (sv + bsp_ref[:, lanes])
            y_ref[rows, D_POOL + p * 2 * HEAD:D_POOL + (p + 1) * 2 * HEAD] = yb.astype(_BF16)

    o_ref[...] = x + gate * _dot(y_ref[...], wout_ref[...])


def _mix_call(x, mod, g, w_in, w_pool, pool_scale, ln_g, ln_b, ws_cat, b_tile, w_out):
    bsz, s, d = x.shape
    const2 = lambda b, j: (0, 0)
    const3 = lambda b, j: (0, 0, 0)
    return pl.pallas_call(
        _mix_kernel,
        grid=(bsz, s // TM_MIX),
        in_specs=[
            pl.BlockSpec((None, TM_MIX, d), lambda b, j: (b, j, 0)),
            pl.BlockSpec((None, N_SUB * N_MOD, d), lambda b, j: (b, 0, 0)),
            pl.BlockSpec((1, d), const2),
            pl.BlockSpec((d, D_POOL + 2 * D_GMLP), const2),
            pl.BlockSpec((len(POOL_WINDOWS), POOL_GROUP, POOL_GROUP), const3),
            pl.BlockSpec((1, D_POOL), const2),
            pl.BlockSpec((1, D_GMLP), const2),
            pl.BlockSpec((1, D_GMLP), const2),
            pl.BlockSpec((CHUNK, N_HEADS * CHUNK), const2),
            pl.BlockSpec((CHUNK, D_GMLP), const2),
            pl.BlockSpec((D_POOL + D_GMLP, d), const2),
        ],
        out_specs=pl.BlockSpec((None, TM_MIX, d), lambda b, j: (b, j, 0)),
        out_shape=jax.ShapeDtypeStruct(x.shape, _F32),
        scratch_shapes=[pltpu.VMEM((HALO + TM_MIX, D_POOL), _F32),
                        pltpu.VMEM((TM_MIX, D_POOL + D_GMLP), _BF16)],
        compiler_params=pltpu.CompilerParams(
            dimension_semantics=("arbitrary", "arbitrary"),
            vmem_limit_bytes=VMEM_LIMIT_BYTES),
        name="mixer",
    )(x, mod, g.reshape(1, d), w_in, w_pool, pool_scale.reshape(1, D_POOL),
      ln_g.reshape(1, D_GMLP), ln_b.reshape(1, D_GMLP), ws_cat, b_tile, w_out)


def kernel(x, c, w_ada, b_ada, norm_ffn1_g, ffn1_w_in, ffn1_w_out, norm_mix_g, w_mix_in, w_pool,
           pool_scale, gmlp_ln_g, gmlp_ln_b, w_spatial, b_spatial, w_mix_out, norm_ffn2_g,
           ffn2_w_in, ffn2_w_out, norm_final_g):
    bsz = x.shape[0]
    depth = w_ada.shape[0]
    for l in range(depth):
        last = l == depth - 1
        mod = _ada_call(c, w_ada[l], b_ada[l]).reshape(bsz, N_SUB * N_MOD, D_MODEL)
        ws_cat = jnp.transpose(w_spatial[l], (1, 0, 2)).reshape(CHUNK, N_HEADS * CHUNK)
        b_tile = jnp.repeat(jnp.transpose(b_spatial[l]), HEAD, axis=1)
        x = _ffn_call(x, mod, norm_ffn1_g[l], ffn1_w_in[l].astype(_BF16), ffn1_w_out[l].astype(_BF16),
                      norm_final_g, sub=0, final=False)
        x = _mix_call(x, mod, norm_mix_g[l], w_mix_in[l].astype(_BF16), w_pool[l].astype(_BF16),
                      pool_scale[l], gmlp_ln_g[l], gmlp_ln_b[l], ws_cat, b_tile,
                      w_mix_out[l].astype(_BF16))
        x = _ffn_call(x, mod, norm_ffn2_g[l], ffn2_w_in[l].astype(_BF16), ffn2_w_out[l].astype(_BF16),
                      norm_final_g, sub=2, final=last)
    if depth == 0:
        raise ValueError("depth must be positive")
    return x
```
